```python
import math
import jax, jax.numpy as jnp
from jax import lax
import numpy as np


D_MODEL = 1024
BATCH = 8
SEQ = 4096
DEPTH = 2
DEC_BATCH = 4
DEC_SEQ = 4096
PAST_LEN = 128

GROUP_DIM = 128
MIX_WIDTH = D_MODEL
RET_HEADS = 4
RET_DK = 128
RET_DV = 128
RET_WIDTH = RET_HEADS * RET_DV
RET_CHUNK = 128
ROPE_BASE = 10000.0
CONV_WIDTH = MIX_WIDTH - RET_WIDTH
CONV_K = 3
FOURIER_GROUPS = 4
FOURIER_WIDTH = FOURIER_GROUPS * GROUP_DIM
SGU_GROUPS = 4
SGU_WIDTH = SGU_GROUPS * GROUP_DIM
SGU_CHUNK = 128
D_FF = 2816
AB_IN = 4 * RET_WIDTH + 3 * CONV_WIDTH
CD_IN = FOURIER_WIDTH + 2 * SGU_WIDTH
N_EVEN = (DEPTH + 1) // 2
N_ODD = DEPTH // 2
DN_ALPHA = (2.0 * DEPTH) ** 0.25
DN_BETA = (8.0 * DEPTH) ** -0.25
LN_EPS = 1e-5

kernel_name = 'hybrid_retention_conv_fourier_sgu_encoder'


def _layer_norm(x, g, b):
    xf = x.astype(jnp.float32)
    mu = jnp.mean(xf, -1, keepdims=True)
    var = jnp.mean(jnp.square(xf - mu), -1, keepdims=True)
    y = (xf - mu) * lax.rsqrt(var + LN_EPS) * g.astype(jnp.float32) + b.astype(jnp.float32)
    return y.astype(x.dtype)


def _swiglu(x, wg, wu, wd):
    return (jax.nn.silu(x @ wg) * (x @ wu)) @ wd


def _rotary(t, cos, sin):
    t1, t2 = jnp.split(t, 2, axis=-1)
    return jnp.concatenate([t1 * cos - t2 * sin, t2 * cos + t1 * sin], axis=-1)


def _retention_one_direction(q, k, v, log_gamma, include_diag):
    bsz, s, h, dk = q.shape
    dv = v.shape[-1]
    c = RET_CHUNK
    n = s // c

    def chunks(t):
        return t.reshape(bsz, n, c, h, t.shape[-1]).transpose(1, 0, 3, 2, 4)

    idx = jnp.arange(c, dtype=jnp.float32)
    diff = idx[:, None] - idx[None, :]
    mask = (diff >= 0) if include_diag else (diff > 0)
    intra = jnp.where(mask, jnp.exp(log_gamma[:, None, None] * jnp.maximum(diff, 0.0)), 0.0)
    q_decay = jnp.exp(log_gamma[:, None] * (idx + 1.0))[..., None]
    k_decay = jnp.exp(log_gamma[:, None] * (c - 1.0 - idx))[..., None]
    chunk_decay = jnp.exp(log_gamma * c)[:, None, None]

    def step(state, qkv):
        qc, kc, vc = qkv
        scores = jnp.einsum('bhid,bhjd->bhij', qc, kc) * intra
        out = (jnp.einsum('bhij,bhjv->bhiv', scores, vc)
               + jnp.einsum('bhid,bhdv->bhiv', qc * q_decay, state))
        state = state * chunk_decay + jnp.einsum('bhjd,bhjv->bhdv', kc * k_decay, vc)
        return state, out

    state0 = jnp.zeros((bsz, h, dk, dv), jnp.float32)
    _, out = lax.scan(step, state0, (chunks(q), chunks(k), chunks(v)))
    return out.transpose(1, 0, 3, 2, 4).reshape(bsz, s, h, dv)


def _mixer_retention_conv(x, w_in, w_out, decay_fwd, decay_bwd, conv_w):
    bsz, s, _ = x.shape
    proj = x @ w_in
    q, k, v, g, gate_b, gate_c, hc = jnp.split(
        proj, [RET_WIDTH, 2 * RET_WIDTH, 3 * RET_WIDTH, 4 * RET_WIDTH,
               4 * RET_WIDTH + CONV_WIDTH, 4 * RET_WIDTH + 2 * CONV_WIDTH], axis=-1)
    pos = jnp.arange(s, dtype=jnp.float32)
    inv_freq = ROPE_BASE ** (-jnp.arange(0, RET_DK, 2, dtype=jnp.float32) / RET_DK)
    ang = pos[:, None] * inv_freq[None, :]
    cos = jnp.cos(ang)[:, None, :]
    sin = jnp.sin(ang)[:, None, :]
    qh = _rotary(q.reshape(bsz, s, RET_HEADS, RET_DK).astype(jnp.float32), cos, sin)
    kh = _rotary(k.reshape(bsz, s, RET_HEADS, RET_DK).astype(jnp.float32), cos, sin) * (RET_DK ** -0.5)
    vh = v.reshape(bsz, s, RET_HEADS, RET_DV).astype(jnp.float32)
    lg_f = jax.nn.log_sigmoid(decay_fwd.astype(jnp.float32))
    lg_b = jax.nn.log_sigmoid(decay_bwd.astype(jnp.float32))
    o_fwd = _retention_one_direction(qh, kh, vh, lg_f, True)
    o_bwd = jnp.flip(_retention_one_direction(jnp.flip(qh, 1), jnp.flip(kh, 1), jnp.flip(vh, 1), lg_b, False), 1)
    o = o_fwd + o_bwd
    mu = jnp.mean(o, -1, keepdims=True)
    var = jnp.mean(jnp.square(o - mu), -1, keepdims=True)
    o = ((o - mu) * lax.rsqrt(var + LN_EPS)).reshape(bsz, s, RET_WIDTH).astype(x.dtype)
    ret_out = jax.nn.silu(g) * o
    z = gate_c * hc
    zp = jnp.pad(z, ((0, 0), (1, 1), (0, 0)))
    conv = conv_w[0] * zp[:, :-2] + conv_w[1] * zp[:, 1:-1] + conv_w[2] * zp[:, 2:]
    conv_out = gate_b * conv
    return jnp.concatenate([ret_out, conv_out], axis=-1) @ w_out


def _mixer_fourier_sgu(x, w_in, w_out, sgu_ln_g, sgu_ln_b, sgu_w, sgu_b):
    bsz, s, _ = x.shape
    proj = x @ w_in
    f_in, u, v = jnp.split(proj, [FOURIER_WIDTH, FOURIER_WIDTH + SGU_WIDTH], axis=-1)
    f = f_in.reshape(bsz, s, FOURIER_GROUPS, GROUP_DIM).astype(jnp.float32)
    f_out = jnp.real(jnp.fft.fft2(f, axes=(1, 3), norm='ortho')).reshape(bsz, s, FOURIER_WIDTH).astype(x.dtype)
    u = jax.nn.gelu(u)
    v = _layer_norm(jax.nn.gelu(v), sgu_ln_g, sgu_ln_b)
    vc = v.reshape(bsz, s // SGU_CHUNK, SGU_CHUNK, SGU_GROUPS, GROUP_DIM)
    sg = jnp.einsum('gij,bnjgc->bnigc', sgu_w, vc) + sgu_b.T[:, :, None]
    d_out = u * sg.reshape(bsz, s, SGU_WIDTH)
    return jnp.concatenate([f_out, d_out], axis=-1) @ w_out


def _trunk(x, ffn_w_gate, ffn_w_up, ffn_w_down, ln_g, ln_b, ab_w_in, ab_w_out, ret_decay_fwd,
           ret_decay_bwd, conv_w, cd_w_in, cd_w_out, sgu_ln_g, sgu_ln_b, sgu_w, sgu_b):
    for layer in range(DEPTH):
        ffn1 = _swiglu(x, ffn_w_gate[layer, 0], ffn_w_up[layer, 0], ffn_w_down[layer, 0])
        x = _layer_norm(DN_ALPHA * x + 0.5 * ffn1, ln_g[layer, 0], ln_b[layer, 0])
        i = layer // 2
        if layer % 2 == 0:
            mix = _mixer_retention_conv(x, ab_w_in[i], ab_w_out[i], ret_decay_fwd[i], ret_decay_bwd[i], conv_w[i])
        else:
            mix = _mixer_fourier_sgu(x, cd_w_in[i], cd_w_out[i], sgu_ln_g[i], sgu_ln_b[i], sgu_w[i], sgu_b[i])
        x = _layer_norm(DN_ALPHA * x + mix, ln_g[layer, 1], ln_b[layer, 1])
        ffn2 = _swiglu(x, ffn_w_gate[layer, 1], ffn_w_up[layer, 1], ffn_w_down[layer, 1])
        x = _layer_norm(DN_ALPHA * x + 0.5 * ffn2, ln_g[layer, 2], ln_b[layer, 2])
    return x


def setup_inputs(seed: int = 0) -> dict:
    key = jax.random.key(seed)
    ks = jax.random.split(key, 20)
    f32 = jnp.float32
    nrm = lambda k, shape, scale: jax.random.normal(k, shape, f32) * scale
    heads = jnp.arange(RET_HEADS, dtype=f32)
    decay_base = jnp.log(2.0 ** (5.0 + heads) - 1.0)
    return {
        'x_prompt': nrm(ks[0], (BATCH, SEQ, D_MODEL), 1.0),
        'x_sample': nrm(ks[1], (DEC_BATCH, DEC_SEQ, D_MODEL), 1.0),
        'ffn_w_gate': nrm(ks[2], (DEPTH, 2, D_MODEL, D_FF), D_MODEL ** -0.5),
        'ffn_w_up': nrm(ks[3], (DEPTH, 2, D_MODEL, D_FF), D_MODEL ** -0.5),
        'ffn_w_down': nrm(ks[4], (DEPTH, 2, D_FF, D_MODEL), DN_BETA * D_FF ** -0.5),
        'ln_g': 1.0 + nrm(ks[5], (DEPTH, 3, D_MODEL), 0.02),
        'ln_b': nrm(ks[6], (DEPTH, 3, D_MODEL), 0.02),
        'ab_w_in': nrm(ks[7], (N_EVEN, D_MODEL, AB_IN), D_MODEL ** -0.5),
        'ab_w_out': nrm(ks[8], (N_EVEN, MIX_WIDTH, D_MODEL), DN_BETA * MIX_WIDTH ** -0.5),
        'ret_decay_fwd': decay_base[None, :] + nrm(ks[9], (N_EVEN, RET_HEADS), 0.1),
        'ret_decay_bwd': decay_base[None, :] + nrm(ks[10], (N_EVEN, RET_HEADS), 0.1),
        'conv_w': nrm(ks[11], (N_EVEN, CONV_K, CONV_WIDTH), CONV_K ** -0.5),
        'cd_w_in': nrm(ks[12], (N_ODD, D_MODEL, CD_IN), D_MODEL ** -0.5),
        'cd_w_out': nrm(ks[13], (N_ODD, MIX_WIDTH, D_MODEL), DN_BETA * MIX_WIDTH ** -0.5),
        'sgu_ln_g': 1.0 + nrm(ks[14], (N_ODD, SGU_WIDTH), 0.02),
        'sgu_ln_b': nrm(ks[15], (N_ODD, SGU_WIDTH), 0.02),
        'sgu_w': nrm(ks[16], (N_ODD, SGU_GROUPS, SGU_CHUNK, SGU_CHUNK), SGU_CHUNK ** -0.5),
        'sgu_b': 1.0 + nrm(ks[17], (N_ODD, SGU_GROUPS, SGU_CHUNK), 0.02),
    }


def reference(x_prompt, x_sample, ffn_w_gate, ffn_w_up, ffn_w_down, ln_g, ln_b, ab_w_in, ab_w_out,
              ret_decay_fwd, ret_decay_bwd, conv_w, cd_w_in, cd_w_out, sgu_ln_g, sgu_ln_b, sgu_w, sgu_b):
    y_prompt = _trunk(x_prompt, ffn_w_gate, ffn_w_up, ffn_w_down, ln_g, ln_b, ab_w_in, ab_w_out,
                      ret_decay_fwd, ret_decay_bwd, conv_w, cd_w_in, cd_w_out, sgu_ln_g, sgu_ln_b, sgu_w, sgu_b)
    y_sample = _trunk(x_sample, ffn_w_gate, ffn_w_up, ffn_w_down, ln_g, ln_b, ab_w_in, ab_w_out,
                      ret_decay_fwd, ret_decay_bwd, conv_w, cd_w_in, cd_w_out, sgu_ln_g, sgu_ln_b, sgu_w, sgu_b)
    return (y_prompt, y_sample)
```

```python
import functools
import math

import jax
import jax.numpy as jnp
from jax import lax
from jax.experimental import pallas as pl
from jax.experimental.pallas import tpu as pltpu

F32 = jnp.float32
BF16 = jnp.bfloat16

D_MODEL = 1024
DEPTH = 2
GROUP = 128
HEADS = 4
BRANCH = HEADS * GROUP
CHUNK = 128
ROPE_BASE = 10000.0
D_FF = 2816
DN_ALPHA = (2.0 * DEPTH) ** 0.25
LN_EPS = 1e-5

TOKEN_TILE = 512
HALO_ROWS = 8
VMEM_LIMIT_BYTES = 56 * 1024 * 1024


def _params(n_axes):
    return pltpu.CompilerParams(
        dimension_semantics=("arbitrary",) * n_axes,
        vmem_limit_bytes=VMEM_LIMIT_BYTES)


def _dot(a, b):
    return jnp.dot(a.astype(BF16), b.astype(BF16), preferred_element_type=F32)


def _dot_nt(a, b):
    return lax.dot_general(a.astype(BF16), b.astype(BF16),
                           (((1,), (1,)), ((), ())), preferred_element_type=F32)


def _dot_tn(a, b):
    return lax.dot_general(a.astype(BF16), b.astype(BF16),
                           (((0,), (0,)), ((), ())), preferred_element_type=F32)


def _normalize(z):
    mu = jnp.mean(z, axis=-1, keepdims=True)
    zc = z - mu
    var = jnp.mean(zc * zc, axis=-1, keepdims=True)
    return zc * lax.rsqrt(var + LN_EPS)


def _layer_norm(z, g, b):
    return _normalize(z) * g + b


def _log_sigmoid(x):
    return jnp.minimum(x, 0.0) - jnp.log1p(jnp.exp(-jnp.abs(x)))


def _const_spec(shape):
    return pl.BlockSpec(shape, lambda *_: (0,) * len(shape))


def _ffn_kernel(x_ref, wg_ref, wu_ref, wd_ref, g_ref, b_ref, o_ref):
    x = x_ref[...]
    xb = x.astype(BF16)
    gate = jnp.dot(xb, wg_ref[...], preferred_element_type=F32)
    up = jnp.dot(xb, wu_ref[...], preferred_element_type=F32)
    h = (gate * jax.nn.sigmoid(gate)) * up
    y = jnp.dot(h.astype(BF16), wd_ref[...], preferred_element_type=F32)
    o_ref[...] = _layer_norm(DN_ALPHA * x + 0.5 * y, g_ref[...], b_ref[...])


def _ffn(x, wg, wu, wd, g, b):
    t = x.shape[0]
    tm = TOKEN_TILE
    single = pl.Buffered(1)
    return pl.pallas_call(
        _ffn_kernel,
        out_shape=jax.ShapeDtypeStruct((t, D_MODEL), F32),
        grid=(t // tm,),
        in_specs=[
            pl.BlockSpec((tm, D_MODEL), lambda i: (i, 0)),
            pl.BlockSpec((D_MODEL, D_FF), lambda i: (0, 0), pipeline_mode=single),
            pl.BlockSpec((D_MODEL, D_FF), lambda i: (0, 0), pipeline_mode=single),
            pl.BlockSpec((D_FF, D_MODEL), lambda i: (0, 0), pipeline_mode=single),
            _const_spec((1, D_MODEL)),
            _const_spec((1, D_MODEL)),
        ],
        out_specs=pl.BlockSpec((tm, D_MODEL), lambda i: (i, 0)),
        compiler_params=_params(1),
        name="ffn_ln",
    )(x, wg, wu, wd, g, b)


def _decay_tables(dec_row, backward):
    lg = _log_sigmoid(dec_row)
    row = lax.broadcasted_iota(jnp.int32, (CHUNK, CHUNK), 0).astype(F32)
    col = lax.broadcasted_iota(jnp.int32, (CHUNK, CHUNK), 1).astype(F32)
    if backward:
        dist = col - row
        intra = jnp.where(dist > 0, jnp.exp(lg * jnp.maximum(dist, 0.0)), 0.0)
        q_decay = jnp.exp(lg * (CHUNK - row))
        k_decay = jnp.exp(lg * row)
    else:
        dist = row - col
        intra = jnp.where(dist >= 0, jnp.exp(lg * jnp.maximum(dist, 0.0)), 0.0)
        q_decay = jnp.exp(lg * (row + 1.0))
        k_decay = jnp.exp(lg * (CHUNK - 1.0 - row))
    chunk_decay = jnp.exp(lg * float(CHUNK))
    return intra, q_decay, k_decay, chunk_decay


def _retention_chunk(qc, kc, vc, state, tables):
    intra, q_decay, k_decay, chunk_decay = tables
    scores = _dot_nt(qc, kc) * intra
    out = _dot(scores, vc) + q_decay * _dot(qc, state)
    new_state = state * chunk_decay + _dot_tn(kc * k_decay, vc)
    return out, new_state


def _rotary(t, cos_full, sin_signed):
    return t * cos_full + pltpu.roll(t, GROUP // 2, 1) * sin_signed


def _mix0_fwd_kernel(x_ref, w_ref, cos_ref, sin_ref, dec_ref,
                     q_ref, k_ref, v_ref, g_ref, z_ref, gb_ref, of_ref, state_ref):
    @pl.when(pl.program_id(1) == 0)
    def _():
        state_ref[...] = jnp.zeros_like(state_ref)

    xb = x_ref[...].astype(BF16)

    def seg(i):
        return jnp.dot(xb, w_ref[:, i * BRANCH:(i + 1) * BRANCH], preferred_element_type=F32)

    q = seg(0)
    k = seg(1)
    v = seg(2)
    g_ref[...] = seg(3)
    gb_ref[...] = seg(4)
    z_ref[...] = seg(5) * seg(6)
    v_ref[...] = v

    cos_full = cos_ref[...]
    sin_signed = sin_ref[...]
    n_chunks = x_ref.shape[0] // CHUNK
    for h in range(HEADS):
        cols = slice(h * GROUP, (h + 1) * GROUP)
        qh = _rotary(q[:, cols], cos_full, sin_signed)
        kh = _rotary(k[:, cols], cos_full, sin_signed) * (GROUP ** -0.5)
        q_ref[:, cols] = qh
        k_ref[:, cols] = kh
        tables = _decay_tables(dec_ref[h:h + 1, :], backward=False)
        state = state_ref[h]
        for c in range(n_chunks):
            rows = slice(c * CHUNK, (c + 1) * CHUNK)
            out, state = _retention_chunk(qh[rows], kh[rows], v[rows, cols], state, tables)
            of_ref[rows, cols] = out
        state_ref[h] = state


def _mix0_fwd(x, w_in, cos_full, sin_signed, dec_fwd, batch, seq):
    t = x.shape[0]
    tm = TOKEN_TILE
    nb = seq // tm
    tok = lambda b, j: (b * nb + j, 0)
    out = jax.ShapeDtypeStruct((t, BRANCH), F32)
    return pl.pallas_call(
        _mix0_fwd_kernel,
        out_shape=(out,) * 7,
        grid=(batch, nb),
        in_specs=[
            pl.BlockSpec((tm, D_MODEL), tok),
            _const_spec(w_in.shape),
            pl.BlockSpec((tm, GROUP), lambda b, j: (j, 0)),
            pl.BlockSpec((tm, GROUP), lambda b, j: (j, 0)),
            _const_spec((HEADS, GROUP)),
        ],
        out_specs=(pl.BlockSpec((tm, BRANCH), tok),) * 7,
        scratch_shapes=[pltpu.VMEM((HEADS, GROUP, GROUP), F32)],
        compiler_params=_params(2),
        name="mix0_fwd",
    )(x, w_in, cos_full, sin_signed, dec_fwd)


def _mix0_bwd_kernel(q_ref, k_ref, v_ref, g_ref, z_ref, zprev_ref, znext_ref, gb_ref, of_ref,
                     x_ref, wo_ref, cw_ref, dec_ref, lng_ref, lnb_ref, o_ref,
                     state_ref, ret_ref):
    j = pl.program_id(1)
    last = pl.num_programs(1) - 1

    @pl.when(j == 0)
    def _():
        state_ref[...] = jnp.zeros_like(state_ref)

    n_chunks = x_ref.shape[0] // CHUNK
    for h in range(HEADS):
        cols = slice(h * GROUP, (h + 1) * GROUP)
        tables = _decay_tables(dec_ref[h:h + 1, :], backward=True)
        state = state_ref[h]
        for c in reversed(range(n_chunks)):
            rows = slice(c * CHUNK, (c + 1) * CHUNK)
            out, state = _retention_chunk(q_ref[rows, cols], k_ref[rows, cols], v_ref[rows, cols],
                                          state, tables)
            o = _normalize(of_ref[rows, cols] + out)
            ret_ref[rows, cols] = jax.nn.silu(g_ref[rows, cols]) * o
        state_ref[h] = state

    z = z_ref[...]
    tm = z.shape[0]
    row = lax.broadcasted_iota(jnp.int32, z.shape, 0)
    prev_row = jnp.where(j == last, 0.0, zprev_ref[HALO_ROWS - 1:HALO_ROWS, :])
    next_row = jnp.where(j == 0, 0.0, znext_ref[0:1, :])
    z_before = jnp.where(row == 0, prev_row, pltpu.roll(z, 1, 0))
    z_after = jnp.where(row == tm - 1, next_row, pltpu.roll(z, tm - 1, 0))
    conv = cw_ref[0:1, :] * z_before + cw_ref[1:2, :] * z + cw_ref[2:3, :] * z_after
    conv_out = gb_ref[...] * conv

    mix = _dot(ret_ref[...], wo_ref[:BRANCH, :]) + _dot(conv_out, wo_ref[BRANCH:, :])
    o_ref[...] = _layer_norm(DN_ALPHA * x_ref[...] + mix, lng_ref[...], lnb_ref[...])


def _mix0_bwd(q, k, v, g, z, gb, of, x, w_out, conv_w, dec_bwd, ln_g, ln_b, batch, seq):
    t = x.shape[0]
    tm = TOKEN_TILE
    nb = seq // tm
    halo_per_tile = tm // HALO_ROWS
    n_halo = t // HALO_ROWS

    def tok(b, j):
        return (b * nb + (nb - 1 - j), 0)

    def prev_halo(b, j):
        return (jnp.maximum((b * nb + (nb - 1 - j)) * halo_per_tile - 1, 0), 0)

    def next_halo(b, j):
        return (jnp.minimum((b * nb + (nb - j)) * halo_per_tile, n_halo - 1), 0)

    branch = pl.BlockSpec((tm, BRANCH), tok)
    halo = lambda f: pl.BlockSpec((HALO_ROWS, BRANCH), f)
    return pl.pallas_call(
        _mix0_bwd_kernel,
        out_shape=jax.ShapeDtypeStruct((t, D_MODEL), F32),
        grid=(batch, nb),
        in_specs=[
            branch, branch, branch, branch, branch, halo(prev_halo), halo(next_halo), branch, branch,
            pl.BlockSpec((tm, D_MODEL), tok),
            _const_spec(w_out.shape),
            _const_spec(conv_w.shape),
            _const_spec((HEADS, GROUP)),
            _const_spec((1, D_MODEL)),
            _const_spec((1, D_MODEL)),
        ],
        out_specs=pl.BlockSpec((tm, D_MODEL), tok),
        scratch_shapes=[pltpu.VMEM((HEADS, GROUP, GROUP), F32),
                        pltpu.VMEM((tm, BRANCH), F32)],
        compiler_params=_params(2),
        name="mix0_bwd",
    )(q, k, v, g, z, z, z, gb, of, x, w_out, conv_w, dec_bwd, ln_g, ln_b)


def _rotary_tables(seq):
    pos = jnp.arange(seq, dtype=F32)
    inv_freq = ROPE_BASE ** (-jnp.arange(0, GROUP, 2, dtype=F32) / GROUP)
    ang = pos[:, None] * inv_freq[None, :]
    cos = jnp.cos(ang)
    sin = jnp.sin(ang)
    return jnp.concatenate([cos, cos], axis=-1), jnp.concatenate([-sin, sin], axis=-1)


def _mixer_retention_conv(x, w_in, w_out, dec_fwd, dec_bwd, conv_w, ln_g, ln_b, batch, seq):
    cos_full, sin_signed = _rotary_tables(seq)
    lanes = lambda d: jnp.broadcast_to(d[:, None], (HEADS, GROUP))
    q, k, v, g, z, gb, of = _mix0_fwd(x, w_in, cos_full, sin_signed, lanes(dec_fwd), batch, seq)
    return _mix0_bwd(q, k, v, g, z, gb, of, x, w_out, conv_w, lanes(dec_bwd), ln_g, ln_b, batch, seq)


def _dft_tables(n, scale):
    idx = jnp.arange(n, dtype=jnp.int32)
    phase = (idx[:, None] * idx[None, :]) % n
    ang = phase.astype(F32) * (2.0 * math.pi / n)
    return (jnp.cos(ang) * scale).astype(BF16), (jnp.sin(ang) * scale).astype(BF16)


def _mix1_in_kernel(x_ref, w_ref, cs_ref, lng_ref, lnb_ref, sw_ref, sb_ref,
                    ure_ref, uim_ref, d_ref):
    xb = x_ref[...].astype(BF16)

    def seg(i):
        return jnp.dot(xb, w_ref[:, i * BRANCH:(i + 1) * BRANCH], preferred_element_type=F32)

    f = seg(0)
    for gidx in range(HEADS):
        cols = slice(gidx * GROUP, (gidx + 1) * GROUP)
        u = _dot(f[:, cols], cs_ref[...])
        ure_ref[:, cols] = u[:, :GROUP].astype(BF16)
        uim_ref[:, cols] = u[:, GROUP:].astype(BF16)

    u_gate = jax.nn.gelu(seg(1))
    v = _layer_norm(jax.nn.gelu(seg(2)), lng_ref[...], lnb_ref[...])
    n_chunks = x_ref.shape[0] // CHUNK
    for gidx in range(HEADS):
        cols = slice(gidx * GROUP, (gidx + 1) * GROUP)
        for c in range(n_chunks):
            rows = slice(c * CHUNK, (c + 1) * CHUNK)
            sg = _dot(sw_ref[gidx], v[rows, cols]) + sb_ref[gidx]
            d_ref[rows, cols] = u_gate[rows, cols] * sg


def _mix1_in(x, w_in, chan_cs, sgu_ln_g, sgu_ln_b, sgu_w, sgu_b_lanes):
    t = x.shape[0]
    tm = TOKEN_TILE
    tok = lambda i: (i, 0)
    return pl.pallas_call(
        _mix1_in_kernel,
        out_shape=(jax.ShapeDtypeStruct((t, BRANCH), BF16),
                   jax.ShapeDtypeStruct((t, BRANCH), BF16),
                   jax.ShapeDtypeStruct((t, BRANCH), F32)),
        grid=(t // tm,),
        in_specs=[
            pl.BlockSpec((tm, D_MODEL), tok),
            _const_spec(w_in.shape),
            _const_spec(chan_cs.shape),
            _const_spec((1, BRANCH)),
            _const_spec((1, BRANCH)),
            _const_spec(sgu_w.shape),
            _const_spec(sgu_b_lanes.shape),
        ],
        out_specs=(pl.BlockSpec((tm, BRANCH), tok),) * 3,
        compiler_params=_params(1),
        name="mix1_in",
    )(x, w_in, chan_cs, sgu_ln_g, sgu_ln_b, sgu_w, sgu_b_lanes)


def _mix1_out_kernel(cs_ref, ss_ref, ure_ref, uim_ref, d_ref, x_ref, wo_ref, lng_ref, lnb_ref, o_ref):
    f = (jnp.dot(cs_ref[...], ure_ref[...], preferred_element_type=F32)
         + jnp.dot(ss_ref[...], uim_ref[...], preferred_element_type=F32))
    mix = _dot(f, wo_ref[:BRANCH, :]) + _dot(d_ref[...], wo_ref[BRANCH:, :])
    o_ref[...] = _layer_norm(DN_ALPHA * x_ref[...] + mix, lng_ref[...], lnb_ref[...])


def _mix1_out(seq_cos, seq_sin, ure, uim, d, x, w_out, ln_g, ln_b, batch, seq):
    t = x.shape[0]
    tm = TOKEN_TILE
    nb = seq // tm
    tok = lambda b, m: (b * nb + m, 0)
    return pl.pallas_call(
        _mix1_out_kernel,
        out_shape=jax.ShapeDtypeStruct((t, D_MODEL), F32),
        grid=(batch, nb),
        in_specs=[
            pl.BlockSpec((tm, seq), lambda b, m: (m, 0)),
            pl.BlockSpec((tm, seq), lambda b, m: (m, 0)),
            pl.BlockSpec((seq, BRANCH), lambda b, m: (b, 0)),
            pl.BlockSpec((seq, BRANCH), lambda b, m: (b, 0)),
            pl.BlockSpec((tm, BRANCH), tok),
            pl.BlockSpec((tm, D_MODEL), tok),
            _const_spec(w_out.shape),
            _const_spec((1, D_MODEL)),
            _const_spec((1, D_MODEL)),
        ],
        out_specs=pl.BlockSpec((tm, D_MODEL), tok),
        compiler_params=_params(2),
        name="mix1_out",
    )(seq_cos, seq_sin, ure, uim, d, x, w_out, ln_g, ln_b)


def _mixer_fourier_sgu(x, w_in, w_out, sgu_ln_g, sgu_ln_b, sgu_w, sgu_b, ln_g, ln_b, batch, seq):
    chan_cos, chan_sin = _dft_tables(GROUP, GROUP ** -0.5)
    chan_cs = jnp.concatenate([chan_cos, chan_sin], axis=1)
    seq_cos, seq_sin = _dft_tables(seq, seq ** -0.5)
    seq_sin = -seq_sin
    sgu_b_lanes = jnp.broadcast_to(sgu_b[:, :, None], (HEADS, CHUNK, GROUP))
    ure, uim, d = _mix1_in(x, w_in, chan_cs, sgu_ln_g, sgu_ln_b, sgu_w, sgu_b_lanes)
    return _mix1_out(seq_cos, seq_sin, ure, uim, d, x, w_out, ln_g, ln_b, batch, seq)


def _trunk(x3, p):
    batch, seq, _ = x3.shape
    x = x3.reshape(batch * seq, D_MODEL)
    row = lambda a: a.reshape(1, -1)
    for layer in range(DEPTH):
        ffn = lambda x, s: _ffn(x, p["wg"][layer, s], p["wu"][layer, s], p["wd"][layer, s],
                                row(p["ln_g"][layer, 2 * s]), row(p["ln_b"][layer, 2 * s]))
        x = ffn(x, 0)
        i = layer // 2
        ln_g, ln_b = row(p["ln_g"][layer, 1]), row(p["ln_b"][layer, 1])
        if layer % 2 == 0:
            x = _mixer_retention_conv(x, p["ab_w_in"][i], p["ab_w_out"][i], p["dec_fwd"][i],
                                      p["dec_bwd"][i], p["conv_w"][i], ln_g, ln_b, batch, seq)
        else:
            x = _mixer_fourier_sgu(x, p["cd_w_in"][i], p["cd_w_out"][i], row(p["sgu_ln_g"][i]),
                                   row(p["sgu_ln_b"][i]), p["sgu_w"][i], p["sgu_b"][i],
                                   ln_g, ln_b, batch, seq)
        x = ffn(x, 1)
    return x.reshape(batch, seq, D_MODEL)


def kernel(x_prompt, x_sample, ffn_w_gate, ffn_w_up, ffn_w_down, ln_g, ln_b, ab_w_in, ab_w_out,
           ret_decay_fwd, ret_decay_bwd, conv_w, cd_w_in, cd_w_out, sgu_ln_g, sgu_ln_b, sgu_w, sgu_b):
    p = {
        "wg": ffn_w_gate.astype(BF16), "wu": ffn_w_up.astype(BF16), "wd": ffn_w_down.astype(BF16),
        "ln_g": ln_g, "ln_b": ln_b,
        "ab_w_in": ab_w_in.astype(BF16), "ab_w_out": ab_w_out.astype(BF16),
        "dec_fwd": ret_decay_fwd, "dec_bwd": ret_decay_bwd, "conv_w": conv_w,
        "cd_w_in": cd_w_in.astype(BF16), "cd_w_out": cd_w_out.astype(BF16),
        "sgu_ln_g": sgu_ln_g, "sgu_ln_b": sgu_ln_b, "sgu_w": sgu_w.astype(BF16), "sgu_b": sgu_b,
    }
    return _trunk(x_prompt, p), _trunk(x_sample, p)
```

```python
import functools
import math

import jax
import jax.numpy as jnp
from jax import lax
from jax.experimental import pallas as pl
from jax.experimental.pallas import tpu as pltpu

F32 = jnp.float32
BF16 = jnp.bfloat16

D_MODEL = 1024
DEPTH = 2
GROUP = 128
HEADS = 4
BRANCH = HEADS * GROUP
CHUNK = 128
ROPE_BASE = 10000.0
D_FF = 2816
DN_ALPHA = (2.0 * DEPTH) ** 0.25
LN_EPS = 1e-5

TOKEN_TILE = 512
HALO_ROWS = 8
VMEM_LIMIT_BYTES = 56 * 1024 * 1024

SEQ_SLABS = 16
SLAB_ROWS = 256
DFT_UNIT_ROWS = 16
DFT_Q_PER_STEP = 2


def _params(n_axes):
    return pltpu.CompilerParams(
        dimension_semantics=("arbitrary",) * n_axes,
        vmem_limit_bytes=VMEM_LIMIT_BYTES)


def _dot(a, b):
    return jnp.dot(a.astype(BF16), b.astype(BF16), preferred_element_type=F32)


def _dot_nt(a, b):
    return lax.dot_general(a.astype(BF16), b.astype(BF16),
                           (((1,), (1,)), ((), ())), preferred_element_type=F32)


def _dot_tn(a, b):
    return lax.dot_general(a.astype(BF16), b.astype(BF16),
                           (((0,), (0,)), ((), ())), preferred_element_type=F32)


def _normalize(z):
    mu = jnp.mean(z, axis=-1, keepdims=True)
    zc = z - mu
    var = jnp.mean(zc * zc, axis=-1, keepdims=True)
    return zc * lax.rsqrt(var + LN_EPS)


def _layer_norm(z, g, b):
    return _normalize(z) * g + b


def _log_sigmoid(x):
    return jnp.minimum(x, 0.0) - jnp.log1p(jnp.exp(-jnp.abs(x)))


def _const_spec(shape):
    return pl.BlockSpec(shape, lambda *_: (0,) * len(shape))


def _ffn_kernel(x_ref, wg_ref, wu_ref, wd_ref, g_ref, b_ref, o_ref):
    x = x_ref[...]
    xb = x.astype(BF16)
    gate = jnp.dot(xb, wg_ref[...], preferred_element_type=F32)
    up = jnp.dot(xb, wu_ref[...], preferred_element_type=F32)
    h = (gate * jax.nn.sigmoid(gate)) * up
    y = jnp.dot(h.astype(BF16), wd_ref[...], preferred_element_type=F32)
    o_ref[...] = _layer_norm(DN_ALPHA * x + 0.5 * y, g_ref[...], b_ref[...])


def _ffn(x, wg, wu, wd, g, b):
    t = x.shape[0]
    tm = TOKEN_TILE
    single = pl.Buffered(1)
    return pl.pallas_call(
        _ffn_kernel,
        out_shape=jax.ShapeDtypeStruct((t, D_MODEL), F32),
        grid=(t // tm,),
        in_specs=[
            pl.BlockSpec((tm, D_MODEL), lambda i: (i, 0)),
            pl.BlockSpec((D_MODEL, D_FF), lambda i: (0, 0), pipeline_mode=single),
            pl.BlockSpec((D_MODEL, D_FF), lambda i: (0, 0), pipeline_mode=single),
            pl.BlockSpec((D_FF, D_MODEL), lambda i: (0, 0), pipeline_mode=single),
            _const_spec((1, D_MODEL)),
            _const_spec((1, D_MODEL)),
        ],
        out_specs=pl.BlockSpec((tm, D_MODEL), lambda i: (i, 0)),
        compiler_params=_params(1),
        name="ffn_ln",
    )(x, wg, wu, wd, g, b)


def _decay_tables(dec_row, backward):
    lg = _log_sigmoid(dec_row)
    row = lax.broadcasted_iota(jnp.int32, (CHUNK, CHUNK), 0).astype(F32)
    col = lax.broadcasted_iota(jnp.int32, (CHUNK, CHUNK), 1).astype(F32)
    if backward:
        dist = col - row
        intra = jnp.where(dist > 0, jnp.exp(lg * jnp.maximum(dist, 0.0)), 0.0)
        q_decay = jnp.exp(lg * (CHUNK - row))
        k_decay = jnp.exp(lg * row)
    else:
        dist = row - col
        intra = jnp.where(dist >= 0, jnp.exp(lg * jnp.maximum(dist, 0.0)), 0.0)
        q_decay = jnp.exp(lg * (row + 1.0))
        k_decay = jnp.exp(lg * (CHUNK - 1.0 - row))
    chunk_decay = jnp.exp(lg * float(CHUNK))
    return intra, q_decay, k_decay, chunk_decay


def _retention_chunk(qc, kc, vc, state, tables):
    intra, q_decay, k_decay, chunk_decay = tables
    scores = _dot_nt(qc, kc) * intra
    out = _dot(scores, vc) + q_decay * _dot(qc, state)
    new_state = state * chunk_decay + _dot_tn(kc * k_decay, vc)
    return out, new_state


def _rotary(t, cos_full, sin_signed):
    return t * cos_full + pltpu.roll(t, GROUP // 2, 1) * sin_signed


def _mix0_fwd_kernel(x_ref, w_ref, cos_ref, sin_ref, dec_ref,
                     q_ref, k_ref, v_ref, g_ref, z_ref, gb_ref, of_ref, state_ref):
    @pl.when(pl.program_id(1) == 0)
    def _():
        state_ref[...] = jnp.zeros_like(state_ref)

    xb = x_ref[...].astype(BF16)

    def seg(i):
        return jnp.dot(xb, w_ref[:, i * BRANCH:(i + 1) * BRANCH], preferred_element_type=F32)

    q = seg(0)
    k = seg(1)
    v = seg(2)
    g_ref[...] = seg(3)
    gb_ref[...] = seg(4)
    z_ref[...] = seg(5) * seg(6)
    v_ref[...] = v

    cos_full = cos_ref[...]
    sin_signed = sin_ref[...]
    n_chunks = x_ref.shape[0] // CHUNK
    for h in range(HEADS):
        cols = slice(h * GROUP, (h + 1) * GROUP)
        qh = _rotary(q[:, cols], cos_full, sin_signed)
        kh = _rotary(k[:, cols], cos_full, sin_signed) * (GROUP ** -0.5)
        q_ref[:, cols] = qh
        k_ref[:, cols] = kh
        tables = _decay_tables(dec_ref[h:h + 1, :], backward=False)
        state = state_ref[h]
        for c in range(n_chunks):
            rows = slice(c * CHUNK, (c + 1) * CHUNK)
            out, state = _retention_chunk(qh[rows], kh[rows], v[rows, cols], state, tables)
            of_ref[rows, cols] = out
        state_ref[h] = state


def _mix0_fwd(x, w_in, cos_full, sin_signed, dec_fwd, batch, seq):
    t = x.shape[0]
    tm = TOKEN_TILE
    nb = seq // tm
    tok = lambda b, j: (b * nb + j, 0)
    out = jax.ShapeDtypeStruct((t, BRANCH), F32)
    return pl.pallas_call(
        _mix0_fwd_kernel,
        out_shape=(out,) * 7,
        grid=(batch, nb),
        in_specs=[
            pl.BlockSpec((tm, D_MODEL), tok),
            _const_spec(w_in.shape),
            pl.BlockSpec((tm, GROUP), lambda b, j: (j, 0)),
            pl.BlockSpec((tm, GROUP), lambda b, j: (j, 0)),
            _const_spec((HEADS, GROUP)),
        ],
        out_specs=(pl.BlockSpec((tm, BRANCH), tok),) * 7,
        scratch_shapes=[pltpu.VMEM((HEADS, GROUP, GROUP), F32)],
        compiler_params=_params(2),
        name="mix0_fwd",
    )(x, w_in, cos_full, sin_signed, dec_fwd)


def _mix0_bwd_kernel(q_ref, k_ref, v_ref, g_ref, z_ref, zprev_ref, znext_ref, gb_ref, of_ref,
                     x_ref, wo_ref, cw_ref, dec_ref, lng_ref, lnb_ref, o_ref,
                     state_ref, ret_ref):
    j = pl.program_id(1)
    last = pl.num_programs(1) - 1

    @pl.when(j == 0)
    def _():
        state_ref[...] = jnp.zeros_like(state_ref)

    n_chunks = x_ref.shape[0] // CHUNK
    for h in range(HEADS):
        cols = slice(h * GROUP, (h + 1) * GROUP)
        tables = _decay_tables(dec_ref[h:h + 1, :], backward=True)
        state = state_ref[h]
        for c in reversed(range(n_chunks)):
            rows = slice(c * CHUNK, (c + 1) * CHUNK)
            out, state = _retention_chunk(q_ref[rows, cols], k_ref[rows, cols], v_ref[rows, cols],
                                          state, tables)
            o = _normalize(of_ref[rows, cols] + out)
            ret_ref[rows, cols] = jax.nn.silu(g_ref[rows, cols]) * o
        state_ref[h] = state

    z = z_ref[...]
    tm = z.shape[0]
    row = lax.broadcasted_iota(jnp.int32, z.shape, 0)
    prev_row = jnp.where(j == last, 0.0, zprev_ref[HALO_ROWS - 1:HALO_ROWS, :])
    next_row = jnp.where(j == 0, 0.0, znext_ref[0:1, :])
    z_before = jnp.where(row == 0, prev_row, pltpu.roll(z, 1, 0))
    z_after = jnp.where(row == tm - 1, next_row, pltpu.roll(z, tm - 1, 0))
    conv = cw_ref[0:1, :] * z_before + cw_ref[1:2, :] * z + cw_ref[2:3, :] * z_after
    conv_out = gb_ref[...] * conv

    mix = _dot(ret_ref[...], wo_ref[:BRANCH, :]) + _dot(conv_out, wo_ref[BRANCH:, :])
    o_ref[...] = _layer_norm(DN_ALPHA * x_ref[...] + mix, lng_ref[...], lnb_ref[...])


def _mix0_bwd(q, k, v, g, z, gb, of, x, w_out, conv_w, dec_bwd, ln_g, ln_b, batch, seq):
    t = x.shape[0]
    tm = TOKEN_TILE
    nb = seq // tm
    halo_per_tile = tm // HALO_ROWS
    n_halo = t // HALO_ROWS

    def tok(b, j):
        return (b * nb + (nb - 1 - j), 0)

    def prev_halo(b, j):
        return (jnp.maximum((b * nb + (nb - 1 - j)) * halo_per_tile - 1, 0), 0)

    def next_halo(b, j):
        return (jnp.minimum((b * nb + (nb - j)) * halo_per_tile, n_halo - 1), 0)

    branch = pl.BlockSpec((tm, BRANCH), tok)
    halo = lambda f: pl.BlockSpec((HALO_ROWS, BRANCH), f)
    return pl.pallas_call(
        _mix0_bwd_kernel,
        out_shape=jax.ShapeDtypeStruct((t, D_MODEL), F32),
        grid=(batch, nb),
        in_specs=[
            branch, branch, branch, branch, branch, halo(prev_halo), halo(next_halo), branch, branch,
            pl.BlockSpec((tm, D_MODEL), tok),
            _const_spec(w_out.shape),
            _const_spec(conv_w.shape),
            _const_spec((HEADS, GROUP)),
            _const_spec((1, D_MODEL)),
            _const_spec((1, D_MODEL)),
        ],
        out_specs=pl.BlockSpec((tm, D_MODEL), tok),
        scratch_shapes=[pltpu.VMEM((HEADS, GROUP, GROUP), F32),
                        pltpu.VMEM((tm, BRANCH), F32)],
        compiler_params=_params(2),
        name="mix0_bwd",
    )(q, k, v, g, z, z, z, gb, of, x, w_out, conv_w, dec_bwd, ln_g, ln_b)


def _rotary_tables(seq):
    pos = jnp.arange(seq, dtype=F32)
    inv_freq = ROPE_BASE ** (-jnp.arange(0, GROUP, 2, dtype=F32) / GROUP)
    ang = pos[:, None] * inv_freq[None, :]
    cos = jnp.cos(ang)
    sin = jnp.sin(ang)
    return jnp.concatenate([cos, cos], axis=-1), jnp.concatenate([-sin, sin], axis=-1)


def _mixer_retention_conv(x, w_in, w_out, dec_fwd, dec_bwd, conv_w, ln_g, ln_b, batch, seq):
    cos_full, sin_signed = _rotary_tables(seq)
    lanes = lambda d: jnp.broadcast_to(d[:, None], (HEADS, GROUP))
    q, k, v, g, z, gb, of = _mix0_fwd(x, w_in, cos_full, sin_signed, lanes(dec_fwd), batch, seq)
    return _mix0_bwd(q, k, v, g, z, gb, of, x, w_out, conv_w, lanes(dec_bwd), ln_g, ln_b, batch, seq)


def _dft_tables(n, scale):
    idx = jnp.arange(n, dtype=jnp.int32)
    phase = (idx[:, None] * idx[None, :]) % n
    ang = phase.astype(F32) * (2.0 * math.pi / n)
    return (jnp.cos(ang) * scale).astype(BF16), (jnp.sin(ang) * scale).astype(BF16)


def _mix1_in_kernel(x_ref, w_ref, cs_ref, lng_ref, lnb_ref, sw_ref, sb_ref,
                    ure_ref, uim_ref, d_ref):
    xb = x_ref[...].astype(BF16)

    def seg(i):
        return jnp.dot(xb, w_ref[:, i * BRANCH:(i + 1) * BRANCH], preferred_element_type=F32)

    f = seg(0)
    for gidx in range(HEADS):
        cols = slice(gidx * GROUP, (gidx + 1) * GROUP)
        u = _dot(f[:, cols], cs_ref[...])
        ure_ref[:, cols] = u[:, :GROUP].astype(BF16)
        uim_ref[:, cols] = u[:, GROUP:].astype(BF16)

    u_gate = jax.nn.gelu(seg(1))
    v = _layer_norm(jax.nn.gelu(seg(2)), lng_ref[...], lnb_ref[...])
    n_chunks = x_ref.shape[0] // CHUNK
    for gidx in range(HEADS):
        cols = slice(gidx * GROUP, (gidx + 1) * GROUP)
        for c in range(n_chunks):
            rows = slice(c * CHUNK, (c + 1) * CHUNK)
            sg = _dot(sw_ref[gidx], v[rows, cols]) + sb_ref[gidx]
            d_ref[rows, cols] = u_gate[rows, cols] * sg


def _mix1_in(x, w_in, chan_cs, sgu_ln_g, sgu_ln_b, sgu_w, sgu_b_lanes):
    t = x.shape[0]
    tm = TOKEN_TILE
    tok = lambda i: (i, 0)
    return pl.pallas_call(
        _mix1_in_kernel,
        out_shape=(jax.ShapeDtypeStruct((t, BRANCH), BF16),
                   jax.ShapeDtypeStruct((t, BRANCH), BF16),
                   jax.ShapeDtypeStruct((t, BRANCH), F32)),
        grid=(t // tm,),
        in_specs=[
            pl.BlockSpec((tm, D_MODEL), tok),
            _const_spec(w_in.shape),
            _const_spec(chan_cs.shape),
            _const_spec((1, BRANCH)),
            _const_spec((1, BRANCH)),
            _const_spec(sgu_w.shape),
            _const_spec(sgu_b_lanes.shape),
        ],
        out_specs=(pl.BlockSpec((tm, BRANCH), tok),) * 3,
        compiler_params=_params(1),
        name="mix1_in",
    )(x, w_in, chan_cs, sgu_ln_g, sgu_ln_b, sgu_w, sgu_b_lanes)


def _dft4(x0, x1, x2, x3):
    t0 = (x0[0] + x2[0], x0[1] + x2[1])
    t1 = (x0[0] - x2[0], x0[1] - x2[1])
    t2 = (x1[0] + x3[0], x1[1] + x3[1])
    t3 = (x1[0] - x3[0], x1[1] - x3[1])
    return ((t0[0] + t2[0], t0[1] + t2[1]),
            (t1[0] + t3[1], t1[1] - t3[0]),
            (t0[0] - t2[0], t0[1] - t2[1]),
            (t1[0] - t3[1], t1[1] + t3[0]))


def _cmul_conj(x, c, s):
    return (x[0] * c + x[1] * s, x[1] * c - x[0] * s)


def _seq_dft_stage1(ure_ref, uim_ref, twc_ref, tws_ref, tre_ref, tim_ref):
    rows = DFT_UNIT_ROWS
    w16 = [(math.cos(2.0 * math.pi * m / SEQ_SLABS), math.sin(2.0 * math.pi * m / SEQ_SLABS))
           for m in range(SEQ_SLABS)]

    def unit(i, carry):
        r0 = pl.multiple_of(i * rows, rows)
        for lane0 in range(0, BRANCH, GROUP):
            lanes = slice(lane0, lane0 + GROUP)
            x = [(ure_ref[pl.ds(a * SLAB_ROWS + r0, rows), lanes].astype(F32),
                  uim_ref[pl.ds(a * SLAB_ROWS + r0, rows), lanes].astype(F32))
                 for a in range(SEQ_SLABS)]
            z = []
            for a0 in range(4):
                y = _dft4(x[a0], x[4 + a0], x[8 + a0], x[12 + a0])
                z.append([y[q1] if a0 * q1 == 0 else _cmul_conj(y[q1], *w16[a0 * q1])
                          for q1 in range(4)])
            for q1 in range(4):
                t = _dft4(z[0][q1], z[1][q1], z[2][q1], z[3][q1])
                for q2 in range(4):
                    q = q1 + 4 * q2
                    tq = t[q2]
                    if q:
                        tq = _cmul_conj(tq, twc_ref[q, pl.ds(r0, rows), :], tws_ref[q, pl.ds(r0, rows), :])
                    tre_ref[q, pl.ds(r0, rows), lanes] = tq[0].astype(BF16)
                    tim_ref[q, pl.ds(r0, rows), lanes] = tq[1].astype(BF16)
        return carry

    lax.fori_loop(0, SLAB_ROWS // rows, unit, 0)


def _mix1_out_kernel(ure_ref, uim_ref, twc_ref, tws_ref, cs_ref, ss_ref, d_ref, x_ref, wo_ref,
                     lng_ref, lnb_ref, o_ref, tre_ref, tim_ref):
    step = pl.program_id(1)

    @pl.when(step == 0)
    def _():
        _seq_dft_stage1(ure_ref, uim_ref, twc_ref, tws_ref, tre_ref, tim_ref)

    for i in range(DFT_Q_PER_STEP):
        q = step * DFT_Q_PER_STEP + i
        f = (jnp.dot(cs_ref[...], tre_ref[q], preferred_element_type=F32)
             + jnp.dot(ss_ref[...], tim_ref[q], preferred_element_type=F32))
        d = d_ref[:, i * BRANCH:(i + 1) * BRANCH]
        mix = _dot(f, wo_ref[:BRANCH, :]) + _dot(d, wo_ref[BRANCH:, :])
        cols = slice(i * D_MODEL, (i + 1) * D_MODEL)
        o_ref[:, cols] = _layer_norm(DN_ALPHA * x_ref[:, cols] + mix, lng_ref[...], lnb_ref[...])


def _mix1_out(ure, uim, tw_cos, tw_sin, slab_cos, slab_sin, d, x, w_out, ln_g, ln_b, batch, seq):
    assert seq == SEQ_SLABS * SLAB_ROWS
    t = x.shape[0]
    qs = DFT_Q_PER_STEP
    d_v = d.reshape(batch * SLAB_ROWS, SEQ_SLABS * BRANCH)
    x_v = x.reshape(batch * SLAB_ROWS, SEQ_SLABS * D_MODEL)
    whole_seq = pl.BlockSpec((seq, BRANCH), lambda b, s: (b, 0))
    out = pl.pallas_call(
        _mix1_out_kernel,
        out_shape=jax.ShapeDtypeStruct(x_v.shape, F32),
        grid=(batch, SEQ_SLABS // qs),
        in_specs=[
            whole_seq, whole_seq,
            _const_spec(tw_cos.shape), _const_spec(tw_sin.shape),
            _const_spec(slab_cos.shape), _const_spec(slab_sin.shape),
            pl.BlockSpec((SLAB_ROWS, qs * BRANCH), lambda b, s: (b, s)),
            pl.BlockSpec((SLAB_ROWS, qs * D_MODEL), lambda b, s: (b, s)),
            _const_spec(w_out.shape),
            _const_spec((1, D_MODEL)),
            _const_spec((1, D_MODEL)),
        ],
        out_specs=pl.BlockSpec((SLAB_ROWS, qs * D_MODEL), lambda b, s: (b, s)),
        scratch_shapes=[pltpu.VMEM((SEQ_SLABS, SLAB_ROWS, BRANCH), BF16),
                        pltpu.VMEM((SEQ_SLABS, SLAB_ROWS, BRANCH), BF16)],
        compiler_params=_params(2),
        name="mix1_out",
    )(ure, uim, tw_cos, tw_sin, slab_cos, slab_sin, d_v, x_v, w_out, ln_g, ln_b)
    return out.reshape(t, D_MODEL)


def _twiddle_tables(seq):
    q = jnp.arange(SEQ_SLABS, dtype=jnp.int32)[:, None]
    r = jnp.arange(SLAB_ROWS, dtype=jnp.int32)[None, :]
    ang = (q * r).astype(F32) * (2.0 * math.pi / seq)
    shape = (SEQ_SLABS, SLAB_ROWS, GROUP)
    return (jnp.broadcast_to(jnp.cos(ang)[:, :, None], shape),
            jnp.broadcast_to(jnp.sin(ang)[:, :, None], shape))


def _mixer_fourier_sgu(x, w_in, w_out, sgu_ln_g, sgu_ln_b, sgu_w, sgu_b, ln_g, ln_b, batch, seq):
    chan_cos, chan_sin = _dft_tables(GROUP, GROUP ** -0.5)
    chan_cs = jnp.concatenate([chan_cos, -chan_sin], axis=1)
    slab_cos, slab_sin = _dft_tables(SLAB_ROWS, seq ** -0.5)
    tw_cos, tw_sin = _twiddle_tables(seq)
    sgu_b_lanes = jnp.broadcast_to(sgu_b[:, :, None], (HEADS, CHUNK, GROUP))
    ure, uim, d = _mix1_in(x, w_in, chan_cs, sgu_ln_g, sgu_ln_b, sgu_w, sgu_b_lanes)
    return _mix1_out(ure, uim, tw_cos, tw_sin, slab_cos, slab_sin, d, x, w_out, ln_g, ln_b, batch, seq)


def _trunk(x3, p):
    batch, seq, _ = x3.shape
    x = x3.reshape(batch * seq, D_MODEL)
    row = lambda a: a.reshape(1, -1)
    for layer in range(DEPTH):
        ffn = lambda x, s: _ffn(x, p["wg"][layer, s], p["wu"][layer, s], p["wd"][layer, s],
                                row(p["ln_g"][layer, 2 * s]), row(p["ln_b"][layer, 2 * s]))
        x = ffn(x, 0)
        i = layer // 2
        ln_g, ln_b = row(p["ln_g"][layer, 1]), row(p["ln_b"][layer, 1])
        if layer % 2 == 0:
            x = _mixer_retention_conv(x, p["ab_w_in"][i], p["ab_w_out"][i], p["dec_fwd"][i],
                                      p["dec_bwd"][i], p["conv_w"][i], ln_g, ln_b, batch, seq)
        else:
            x = _mixer_fourier_sgu(x, p["cd_w_in"][i], p["cd_w_out"][i], row(p["sgu_ln_g"][i]),
                                   row(p["sgu_ln_b"][i]), p["sgu_w"][i], p["sgu_b"][i],
                                   ln_g, ln_b, batch, seq)
        x = ffn(x, 1)
    return x.reshape(batch, seq, D_MODEL)


def kernel(x_prompt, x_sample, ffn_w_gate, ffn_w_up, ffn_w_down, ln_g, ln_b, ab_w_in, ab_w_out,
           ret_decay_fwd, ret_decay_bwd, conv_w, cd_w_in, cd_w_out, sgu_ln_g, sgu_ln_b, sgu_w, sgu_b):
    p = {
        "wg": ffn_w_gate.astype(BF16), "wu": ffn_w_up.astype(BF16), "wd": ffn_w_down.astype(BF16),
        "ln_g": ln_g, "ln_b": ln_b,
        "ab_w_in": ab_w_in.astype(BF16), "ab_w_out": ab_w_out.astype(BF16),
        "dec_fwd": ret_decay_fwd, "dec_bwd": ret_decay_bwd, "conv_w": conv_w,
        "cd_w_in": cd_w_in.astype(BF16), "cd_w_out": cd_w_out.astype(BF16),
        "sgu_ln_g": sgu_ln_g, "sgu_ln_b": sgu_ln_b, "sgu_w": sgu_w.astype(BF16), "sgu_b": sgu_b,
    }
    return _trunk(x_prompt, p), _trunk(x_sample, p)
```

```python
import functools
import math

import jax
import jax.numpy as jnp
from jax import lax
from jax.experimental import pallas as pl
from jax.experimental.pallas import tpu as pltpu

F32 = jnp.float32
BF16 = jnp.bfloat16

D_MODEL = 1024
DEPTH = 2
GROUP = 128
HEADS = 4
BRANCH = HEADS * GROUP
CHUNK = 128
RET_BLOCK = 256
ROPE_BASE = 10000.0
D_FF = 2816
DN_ALPHA = (2.0 * DEPTH) ** 0.25
LN_EPS = 1e-5

TOKEN_TILE = 512
HALO_ROWS = 8
VMEM_LIMIT_BYTES = 56 * 1024 * 1024

SEQ_SLABS = 16
SLAB_ROWS = 256
DFT_UNIT_ROWS = 16


def _params(n_axes):
    return pltpu.CompilerParams(
        dimension_semantics=("arbitrary",) * n_axes,
        vmem_limit_bytes=VMEM_LIMIT_BYTES)


def _dot(a, b):
    return jnp.dot(a.astype(BF16), b.astype(BF16), preferred_element_type=F32)


def _dot_nt(a, b):
    return lax.dot_general(a.astype(BF16), b.astype(BF16),
                           (((1,), (1,)), ((), ())), preferred_element_type=F32)


def _dot_tn(a, b):
    return lax.dot_general(a.astype(BF16), b.astype(BF16),
                           (((0,), (0,)), ((), ())), preferred_element_type=F32)


def _normalize(z):
    mu = jnp.mean(z, axis=-1, keepdims=True)
    zc = z - mu
    var = jnp.mean(zc * zc, axis=-1, keepdims=True)
    return zc * lax.rsqrt(var + LN_EPS)


def _layer_norm(z, g, b):
    return _normalize(z) * g + b


def _log_sigmoid(x):
    return jnp.minimum(x, 0.0) - jnp.log1p(jnp.exp(-jnp.abs(x)))


def _const_spec(shape):
    return pl.BlockSpec(shape, lambda *_: (0,) * len(shape))


def _ffn_kernel(x_ref, wg_ref, wu_ref, wd_ref, g_ref, b_ref, o_ref):
    x = x_ref[...]
    xb = x.astype(BF16)
    gate = jnp.dot(xb, wg_ref[...], preferred_element_type=F32)
    up = jnp.dot(xb, wu_ref[...], preferred_element_type=F32)
    h = (gate * jax.nn.sigmoid(gate)) * up
    y = jnp.dot(h.astype(BF16), wd_ref[...], preferred_element_type=F32)
    o_ref[...] = _layer_norm(DN_ALPHA * x + 0.5 * y, g_ref[...], b_ref[...])


def _ffn(x, wg, wu, wd, g, b):
    t = x.shape[0]
    tm = TOKEN_TILE
    single = pl.Buffered(1)
    return pl.pallas_call(
        _ffn_kernel,
        out_shape=jax.ShapeDtypeStruct((t, D_MODEL), F32),
        grid=(t // tm,),
        in_specs=[
            pl.BlockSpec((tm, D_MODEL), lambda i: (i, 0)),
            pl.BlockSpec((D_MODEL, D_FF), lambda i: (0, 0), pipeline_mode=single),
            pl.BlockSpec((D_MODEL, D_FF), lambda i: (0, 0), pipeline_mode=single),
            pl.BlockSpec((D_FF, D_MODEL), lambda i: (0, 0), pipeline_mode=single),
            _const_spec((1, D_MODEL)),
            _const_spec((1, D_MODEL)),
        ],
        out_specs=pl.BlockSpec((tm, D_MODEL), lambda i: (i, 0)),
        compiler_params=_params(1),
        name="ffn_ln",
    )(x, wg, wu, wd, g, b)


def _block_positions():
    return lax.broadcasted_iota(jnp.int32, (RET_BLOCK, GROUP), 0).astype(F32)


def _rotary(t, cos_full, sin_signed):
    return t * cos_full + pltpu.roll(t, GROUP // 2, 1) * sin_signed


def _mix0_in_kernel(x_ref, w_ref, cos_ref, sin_ref, dec_ref,
                    q_ref, k_ref, v_ref, g_ref, z_ref, gb_ref, sf_ref, state_ref):
    @pl.when(pl.program_id(1) == 0)
    def _():
        state_ref[...] = jnp.zeros_like(state_ref)

    xb = x_ref[...].astype(BF16)

    def seg(i):
        return jnp.dot(xb, w_ref[:, i * BRANCH:(i + 1) * BRANCH], preferred_element_type=F32)

    q = seg(0)
    k = seg(1)
    v = seg(2)
    g_ref[...] = seg(3)
    gb_ref[...] = seg(4)
    z_ref[...] = seg(5) * seg(6)
    v_ref[...] = v.astype(BF16)

    cos_full = cos_ref[...]
    sin_signed = sin_ref[...]
    pos = _block_positions()
    n_blocks = x_ref.shape[0] // RET_BLOCK
    for h in range(HEADS):
        cols = slice(h * GROUP, (h + 1) * GROUP)
        q_ref[:, cols] = _rotary(q[:, cols], cos_full, sin_signed).astype(BF16)
        kh = _rotary(k[:, cols], cos_full, sin_signed) * (GROUP ** -0.5)
        k_ref[:, cols] = kh.astype(BF16)
        lg = _log_sigmoid(dec_ref[h:h + 1, :])
        k_decay = jnp.exp(lg * (RET_BLOCK - 1.0 - pos))
        block_decay = jnp.exp(lg * float(RET_BLOCK))
        state = state_ref[h]
        for c in range(n_blocks):
            rows = slice(c * RET_BLOCK, (c + 1) * RET_BLOCK)
            sf_ref[c * GROUP:(c + 1) * GROUP, cols] = state.astype(BF16)
            state = state * block_decay + _dot_tn(kh[rows] * k_decay, v[rows, cols])
        state_ref[h] = state


def _mix0_in(x, w_in, cos_full, sin_signed, dec_fwd, batch, seq):
    t = x.shape[0]
    tm = TOKEN_TILE
    nb = seq // tm
    state_rows = tm // RET_BLOCK * GROUP
    tok = lambda b, j: (b * nb + j, 0)
    act = lambda dtype: jax.ShapeDtypeStruct((t, BRANCH), dtype)
    return pl.pallas_call(
        _mix0_in_kernel,
        out_shape=(act(BF16), act(BF16), act(BF16), act(F32), act(F32), act(F32),
                   jax.ShapeDtypeStruct((t // RET_BLOCK * GROUP, BRANCH), BF16)),
        grid=(batch, nb),
        in_specs=[
            pl.BlockSpec((tm, D_MODEL), tok),
            _const_spec(w_in.shape),
            pl.BlockSpec((tm, GROUP), lambda b, j: (j, 0)),
            pl.BlockSpec((tm, GROUP), lambda b, j: (j, 0)),
            _const_spec((HEADS, GROUP)),
        ],
        out_specs=(pl.BlockSpec((tm, BRANCH), tok),) * 6 + (pl.BlockSpec((state_rows, BRANCH), tok),),
        scratch_shapes=[pltpu.VMEM((HEADS, GROUP, GROUP), F32)],
        compiler_params=_params(2),
        name="mix0_in",
    )(x, w_in, cos_full, sin_signed, dec_fwd)


def _mix0_out_kernel(q_ref, k_ref, v_ref, g_ref, z_ref, zprev_ref, znext_ref, gb_ref, sf_ref,
                     x_ref, wo_ref, cw_ref, decf_ref, decb_ref, lng_ref, lnb_ref, o_ref,
                     state_ref, ret_ref, decay_ref, vec_ref):
    j = pl.program_id(1)
    last = pl.num_programs(1) - 1

    @pl.when((pl.program_id(0) == 0) & (j == 0))
    def _():
        pos = _block_positions()
        dist = (lax.broadcasted_iota(jnp.int32, (RET_BLOCK, RET_BLOCK), 0)
                - lax.broadcasted_iota(jnp.int32, (RET_BLOCK, RET_BLOCK), 1)).astype(F32)
        for h in range(HEADS):
            lgf_wide = _log_sigmoid(decf_ref[h:h + 1, :])
            lgb_wide = _log_sigmoid(decb_ref[h:h + 1, :])
            lgf = lgf_wide[:, :GROUP]
            lgb = lgb_wide[:, :GROUP]
            decay_ref[h] = jnp.where(dist >= 0, jnp.exp(lgf_wide * jnp.maximum(dist, 0.0)),
                                     jnp.exp(lgb_wide * jnp.maximum(-dist, 0.0)))
            vec_ref[h, 0] = jnp.exp(lgf * (pos + 1.0))
            vec_ref[h, 1] = jnp.exp(lgb * (RET_BLOCK - pos))
            vec_ref[h, 2] = jnp.exp(lgb * pos)
            vec_ref[h, 3] = jnp.broadcast_to(jnp.exp(lgb * float(RET_BLOCK)), (RET_BLOCK, GROUP))

    @pl.when(j == 0)
    def _():
        state_ref[...] = jnp.zeros_like(state_ref)

    n_blocks = x_ref.shape[0] // RET_BLOCK
    for h in range(HEADS):
        cols = slice(h * GROUP, (h + 1) * GROUP)
        decay = decay_ref[h]
        q_decay_f = vec_ref[h, 0]
        q_decay_b = vec_ref[h, 1]
        k_decay_b = vec_ref[h, 2]
        block_decay_b = vec_ref[h, 3, :GROUP, :]
        state = state_ref[h]
        for c in reversed(range(n_blocks)):
            rows = slice(c * RET_BLOCK, (c + 1) * RET_BLOCK)
            qc = q_ref[rows, cols]
            kc = k_ref[rows, cols]
            vc = v_ref[rows, cols]
            qf = qc.astype(F32)
            scores = _dot_nt(qc, kc) * decay
            lhs = jnp.concatenate([scores.astype(BF16), (qf * q_decay_f).astype(BF16),
                                   (qf * q_decay_b).astype(BF16)], axis=1)
            rhs = jnp.concatenate([vc, sf_ref[c * GROUP:(c + 1) * GROUP, cols], state.astype(BF16)], axis=0)
            out = jnp.dot(lhs, rhs, preferred_element_type=F32)
            state = state * block_decay_b + _dot_tn(kc.astype(F32) * k_decay_b, vc)
            ret_ref[rows, cols] = jax.nn.silu(g_ref[rows, cols]) * _normalize(out)
        state_ref[h] = state

    z = z_ref[...]
    tm = z.shape[0]
    row = lax.broadcasted_iota(jnp.int32, z.shape, 0)
    prev_row = jnp.where(j == last, 0.0, zprev_ref[HALO_ROWS - 1:HALO_ROWS, :])
    next_row = jnp.where(j == 0, 0.0, znext_ref[0:1, :])
    z_before = jnp.where(row == 0, prev_row, pltpu.roll(z, 1, 0))
    z_after = jnp.where(row == tm - 1, next_row, pltpu.roll(z, tm - 1, 0))
    conv = cw_ref[0:1, :] * z_before + cw_ref[1:2, :] * z + cw_ref[2:3, :] * z_after
    conv_out = gb_ref[...] * conv

    mix = _dot(ret_ref[...], wo_ref[:BRANCH, :]) + _dot(conv_out, wo_ref[BRANCH:, :])
    o_ref[...] = _layer_norm(DN_ALPHA * x_ref[...] + mix, lng_ref[...], lnb_ref[...])


def _mix0_out(q, k, v, g, z, gb, sf, x, w_out, conv_w, dec_fwd, dec_bwd, ln_g, ln_b, batch, seq):
    t = x.shape[0]
    tm = TOKEN_TILE
    nb = seq // tm
    halo_per_tile = tm // HALO_ROWS
    n_halo = t // HALO_ROWS
    state_rows = tm // RET_BLOCK * GROUP

    def tok(b, j):
        return (b * nb + (nb - 1 - j), 0)

    def prev_halo(b, j):
        return (jnp.maximum((b * nb + (nb - 1 - j)) * halo_per_tile - 1, 0), 0)

    def next_halo(b, j):
        return (jnp.minimum((b * nb + (nb - j)) * halo_per_tile, n_halo - 1), 0)

    branch = pl.BlockSpec((tm, BRANCH), tok)
    halo = lambda f: pl.BlockSpec((HALO_ROWS, BRANCH), f)
    return pl.pallas_call(
        _mix0_out_kernel,
        out_shape=jax.ShapeDtypeStruct((t, D_MODEL), F32),
        grid=(batch, nb),
        in_specs=[
            branch, branch, branch, branch, branch, halo(prev_halo), halo(next_halo), branch,
            pl.BlockSpec((state_rows, BRANCH), tok),
            pl.BlockSpec((tm, D_MODEL), tok),
            _const_spec(w_out.shape),
            _const_spec(conv_w.shape),
            _const_spec((HEADS, RET_BLOCK)),
            _const_spec((HEADS, RET_BLOCK)),
            _const_spec((1, D_MODEL)),
            _const_spec((1, D_MODEL)),
        ],
        out_specs=pl.BlockSpec((tm, D_MODEL), tok),
        scratch_shapes=[pltpu.VMEM((HEADS, GROUP, GROUP), F32),
                        pltpu.VMEM((tm, BRANCH), F32),
                        pltpu.VMEM((HEADS, RET_BLOCK, RET_BLOCK), F32),
                        pltpu.VMEM((HEADS, 4, RET_BLOCK, GROUP), F32)],
        compiler_params=_params(2),
        name="mix0_out",
    )(q, k, v, g, z, z, z, gb, sf, x, w_out, conv_w, dec_fwd, dec_bwd, ln_g, ln_b)


def _rotary_tables(seq):
    pos = jnp.arange(seq, dtype=F32)
    inv_freq = ROPE_BASE ** (-jnp.arange(0, GROUP, 2, dtype=F32) / GROUP)
    ang = pos[:, None] * inv_freq[None, :]
    cos = jnp.cos(ang)
    sin = jnp.sin(ang)
    return jnp.concatenate([cos, cos], axis=-1), jnp.concatenate([-sin, sin], axis=-1)


def _mixer_retention_conv(x, w_in, w_out, dec_fwd, dec_bwd, conv_w, ln_g, ln_b, batch, seq):
    cos_full, sin_signed = _rotary_tables(seq)
    lanes = lambda d, n: jnp.broadcast_to(d[:, None], (HEADS, n))
    q, k, v, g, z, gb, sf = _mix0_in(x, w_in, cos_full, sin_signed, lanes(dec_fwd, GROUP), batch, seq)
    return _mix0_out(q, k, v, g, z, gb, sf, x, w_out, conv_w, lanes(dec_fwd, RET_BLOCK),
                     lanes(dec_bwd, RET_BLOCK), ln_g, ln_b, batch, seq)


def _dft_tables(n, scale):
    idx = jnp.arange(n, dtype=jnp.int32)
    phase = (idx[:, None] * idx[None, :]) % n
    ang = phase.astype(F32) * (2.0 * math.pi / n)
    return (jnp.cos(ang) * scale).astype(BF16), (jnp.sin(ang) * scale).astype(BF16)


def _mix1_in_kernel(x_ref, w_ref, cs_ref, lng_ref, lnb_ref, sw_ref, sb_ref,
                    ure_ref, uim_ref, d_ref):
    xb = x_ref[...].astype(BF16)

    def seg(i):
        return jnp.dot(xb, w_ref[:, i * BRANCH:(i + 1) * BRANCH], preferred_element_type=F32)

    f = seg(0)
    for gidx in range(HEADS):
        cols = slice(gidx * GROUP, (gidx + 1) * GROUP)
        u = _dot(f[:, cols], cs_ref[...])
        ure_ref[:, cols] = u[:, :GROUP].astype(BF16)
        uim_ref[:, cols] = u[:, GROUP:].astype(BF16)

    u_gate = jax.nn.gelu(seg(1))
    v = _layer_norm(jax.nn.gelu(seg(2)), lng_ref[...], lnb_ref[...])
    n_chunks = x_ref.shape[0] // CHUNK
    for gidx in range(HEADS):
        cols = slice(gidx * GROUP, (gidx + 1) * GROUP)
        for c in range(n_chunks):
            rows = slice(c * CHUNK, (c + 1) * CHUNK)
            sg = _dot(sw_ref[gidx], v[rows, cols]) + sb_ref[gidx]
            d_ref[rows, cols] = u_gate[rows, cols] * sg


def _mix1_in(x, w_in, chan_cs, sgu_ln_g, sgu_ln_b, sgu_w, sgu_b_lanes):
    t = x.shape[0]
    tm = TOKEN_TILE
    tok = lambda i: (i, 0)
    return pl.pallas_call(
        _mix1_in_kernel,
        out_shape=(jax.ShapeDtypeStruct((t, BRANCH), BF16),
                   jax.ShapeDtypeStruct((t, BRANCH), BF16),
                   jax.ShapeDtypeStruct((t, BRANCH), F32)),
        grid=(t // tm,),
        in_specs=[
            pl.BlockSpec((tm, D_MODEL), tok),
            _const_spec(w_in.shape),
            _const_spec(chan_cs.shape),
            _const_spec((1, BRANCH)),
            _const_spec((1, BRANCH)),
            _const_spec(sgu_w.shape),
            _const_spec(sgu_b_lanes.shape),
        ],
        out_specs=(pl.BlockSpec((tm, BRANCH), tok),) * 3,
        compiler_params=_params(1),
        name="mix1_in",
    )(x, w_in, chan_cs, sgu_ln_g, sgu_ln_b, sgu_w, sgu_b_lanes)


def _dft4(x0, x1, x2, x3):
    t0 = (x0[0] + x2[0], x0[1] + x2[1])
    t1 = (x0[0] - x2[0], x0[1] - x2[1])
    t2 = (x1[0] + x3[0], x1[1] + x3[1])
    t3 = (x1[0] - x3[0], x1[1] - x3[1])
    return ((t0[0] + t2[0], t0[1] + t2[1]),
            (t1[0] + t3[1], t1[1] - t3[0]),
            (t0[0] - t2[0], t0[1] - t2[1]),
            (t1[0] - t3[1], t1[1] + t3[0]))


def _cmul_conj(x, c, s):
    return (x[0] * c + x[1] * s, x[1] * c - x[0] * s)


def _seq_dft_stage1(ure_ref, uim_ref, twc_ref, tws_ref, tre_ref, tim_ref):
    rows = DFT_UNIT_ROWS
    w16 = [(math.cos(2.0 * math.pi * m / SEQ_SLABS), math.sin(2.0 * math.pi * m / SEQ_SLABS))
           for m in range(SEQ_SLABS)]

    def unit(i, carry):
        r0 = pl.multiple_of(i * rows, rows)
        for lane0 in range(0, BRANCH, GROUP):
            lanes = slice(lane0, lane0 + GROUP)
            x = [(ure_ref[pl.ds(a * SLAB_ROWS + r0, rows), lanes].astype(F32),
                  uim_ref[pl.ds(a * SLAB_ROWS + r0, rows), lanes].astype(F32))
                 for a in range(SEQ_SLABS)]
            z = []
            for a0 in range(4):
                y = _dft4(x[a0], x[4 + a0], x[8 + a0], x[12 + a0])
                z.append([y[q1] if a0 * q1 == 0 else _cmul_conj(y[q1], *w16[a0 * q1])
                          for q1 in range(4)])
            for q1 in range(4):
                t = _dft4(z[0][q1], z[1][q1], z[2][q1], z[3][q1])
                for q2 in range(4):
                    q = q1 + 4 * q2
                    tq = t[q2]
                    if q:
                        tq = _cmul_conj(tq, twc_ref[q, pl.ds(r0, rows), :], tws_ref[q, pl.ds(r0, rows), :])
                    tre_ref[q, pl.ds(r0, rows), lanes] = tq[0].astype(BF16)
                    tim_ref[q, pl.ds(r0, rows), lanes] = tq[1].astype(BF16)
        return carry

    lax.fori_loop(0, SLAB_ROWS // rows, unit, 0)


def _mix1_out_kernel(ure_ref, uim_ref, twc_ref, tws_ref, cs_ref, ss_ref, perm_ref, d_ref, x_ref, wo_ref,
                     lng_ref, lnb_ref, o_ref, tre_ref, tim_ref, f_ref):
    j = pl.program_id(1)

    @pl.when(j == 0)
    def _():
        _seq_dft_stage1(ure_ref, uim_ref, twc_ref, tws_ref, tre_ref, tim_ref)
        for q in range(SEQ_SLABS):
            f = (jnp.dot(cs_ref[...], tre_ref[q], preferred_element_type=F32)
                 + jnp.dot(ss_ref[...], tim_ref[q], preferred_element_type=F32))
            f_ref[q] = f.astype(BF16)

    p_rows = x_ref.shape[0] // SEQ_SLABS
    p0 = pl.multiple_of(j * p_rows, p_rows)
    gathered = jnp.concatenate([f_ref[q, pl.ds(p0, p_rows), :] for q in range(SEQ_SLABS)], axis=0)
    f_tile = jnp.dot(perm_ref[...], gathered, preferred_element_type=F32)
    mix = _dot(f_tile, wo_ref[:BRANCH, :]) + _dot(d_ref[...], wo_ref[BRANCH:, :])
    o_ref[...] = _layer_norm(DN_ALPHA * x_ref[...] + mix, lng_ref[...], lnb_ref[...])


def _mix1_out(ure, uim, tw_cos, tw_sin, slab_cos, slab_sin, d, x, w_out, ln_g, ln_b, batch, seq):
    assert seq == SEQ_SLABS * SLAB_ROWS
    t = x.shape[0]
    tm = TOKEN_TILE
    nb = seq // tm
    i = jnp.arange(tm, dtype=jnp.int32)
    src = (i % SEQ_SLABS) * (tm // SEQ_SLABS) + i // SEQ_SLABS
    perm = (src[:, None] == jnp.arange(tm, dtype=jnp.int32)[None, :]).astype(BF16)
    tok = lambda b, j: (b * nb + j, 0)
    whole_seq = pl.BlockSpec((seq, BRANCH), lambda b, j: (b, 0))
    slabs = pltpu.VMEM((SEQ_SLABS, SLAB_ROWS, BRANCH), BF16)
    return pl.pallas_call(
        _mix1_out_kernel,
        out_shape=jax.ShapeDtypeStruct((t, D_MODEL), F32),
        grid=(batch, nb),
        in_specs=[
            whole_seq, whole_seq,
            _const_spec(tw_cos.shape), _const_spec(tw_sin.shape),
            _const_spec(slab_cos.shape), _const_spec(slab_sin.shape),
            _const_spec(perm.shape),
            pl.BlockSpec((tm, BRANCH), tok),
            pl.BlockSpec((tm, D_MODEL), tok),
            _const_spec(w_out.shape),
            _const_spec((1, D_MODEL)),
            _const_spec((1, D_MODEL)),
        ],
        out_specs=pl.BlockSpec((tm, D_MODEL), tok),
        scratch_shapes=[slabs, slabs, slabs],
        compiler_params=_params(2),
        name="mix1_out",
    )(ure, uim, tw_cos, tw_sin, slab_cos, slab_sin, perm, d, x, w_out, ln_g, ln_b)


def _twiddle_tables(seq):
    q = jnp.arange(SEQ_SLABS, dtype=jnp.int32)[:, None]
    r = jnp.arange(SLAB_ROWS, dtype=jnp.int32)[None, :]
    ang = (q * r).astype(F32) * (2.0 * math.pi / seq)
    shape = (SEQ_SLABS, SLAB_ROWS, GROUP)
    return (jnp.broadcast_to(jnp.cos(ang)[:, :, None], shape),
            jnp.broadcast_to(jnp.sin(ang)[:, :, None], shape))


def _mixer_fourier_sgu(x, w_in, w_out, sgu_ln_g, sgu_ln_b, sgu_w, sgu_b, ln_g, ln_b, batch, seq):
    chan_cos, chan_sin = _dft_tables(GROUP, GROUP ** -0.5)
    chan_cs = jnp.concatenate([chan_cos, -chan_sin], axis=1)
    slab_cos, slab_sin = _dft_tables(SLAB_ROWS, seq ** -0.5)
    tw_cos, tw_sin = _twiddle_tables(seq)
    sgu_b_lanes = jnp.broadcast_to(sgu_b[:, :, None], (HEADS, CHUNK, GROUP))
    ure, uim, d = _mix1_in(x, w_in, chan_cs, sgu_ln_g, sgu_ln_b, sgu_w, sgu_b_lanes)
    return _mix1_out(ure, uim, tw_cos, tw_sin, slab_cos, slab_sin, d, x, w_out, ln_g, ln_b, batch, seq)


def _trunk(x3, p):
    batch, seq, _ = x3.shape
    x = x3.reshape(batch * seq, D_MODEL)
    row = lambda a: a.reshape(1, -1)
    for layer in range(DEPTH):
        ffn = lambda x, s: _ffn(x, p["wg"][layer, s], p["wu"][layer, s], p["wd"][layer, s],
                                row(p["ln_g"][layer, 2 * s]), row(p["ln_b"][layer, 2 * s]))
        x = ffn(x, 0)
        i = layer // 2
        ln_g, ln_b = row(p["ln_g"][layer, 1]), row(p["ln_b"][layer, 1])
        if layer % 2 == 0:
            x = _mixer_retention_conv(x, p["ab_w_in"][i], p["ab_w_out"][i], p["dec_fwd"][i],
                                      p["dec_bwd"][i], p["conv_w"][i], ln_g, ln_b, batch, seq)
        else:
            x = _mixer_fourier_sgu(x, p["cd_w_in"][i], p["cd_w_out"][i], row(p["sgu_ln_g"][i]),
                                   row(p["sgu_ln_b"][i]), p["sgu_w"][i], p["sgu_b"][i],
                                   ln_g, ln_b, batch, seq)
        x = ffn(x, 1)
    return x.reshape(batch, seq, D_MODEL)


def kernel(x_prompt, x_sample, ffn_w_gate, ffn_w_up, ffn_w_down, ln_g, ln_b, ab_w_in, ab_w_out,
           ret_decay_fwd, ret_decay_bwd, conv_w, cd_w_in, cd_w_out, sgu_ln_g, sgu_ln_b, sgu_w, sgu_b):
    p = {
        "wg": ffn_w_gate.astype(BF16), "wu": ffn_w_up.astype(BF16), "wd": ffn_w_down.astype(BF16),
        "ln_g": ln_g, "ln_b": ln_b,
        "ab_w_in": ab_w_in.astype(BF16), "ab_w_out": ab_w_out.astype(BF16),
        "dec_fwd": ret_decay_fwd, "dec_bwd": ret_decay_bwd, "conv_w": conv_w,
        "cd_w_in": cd_w_in.astype(BF16), "cd_w_out": cd_w_out.astype(BF16),
        "sgu_ln_g": sgu_ln_g, "sgu_ln_b": sgu_ln_b, "sgu_w": sgu_w.astype(BF16), "sgu_b": sgu_b,
    }
    return _trunk(x_prompt, p), _trunk(x_sample, p)
```
